```python
import math
import jax, jax.numpy as jnp
from jax import lax
import numpy as np

D_MODEL = 2048
BATCH = 4
SEQ = 2048
DEPTH = 1
DEC_BATCH = 128
DEC_SEQ = 4
PAST_LEN = 16384
PAGE_SIZE = 128

RET_HEADS = 8
RET_DK = 128
RET_DV = 256
RET_CHUNK = 128
ROPE_BASE = 10000.0
RET_Q = RET_HEADS * RET_DK
RET_V = RET_HEADS * RET_DV
WKV_HEADS = 16
WKV_N = 64
WKV_DIM = WKV_HEADS * WKV_N
W_LORA = 64
A_LORA = 64
G_LORA = 160
WKV_GN_EPS = 64e-5
RWKV_PROJ = 3 * WKV_DIM + W_LORA + A_LORA + G_LORA
IN_COLS = 2 * RET_Q + 2 * RET_V + RWKV_PROJ + 2 * D_MODEL
N_GROUPS = 4
EXPERTS_PER_GROUP = 8
N_EXPERTS = N_GROUPS * EXPERTS_PER_GROUP
TOP_K = 2
D_EXPERT = 1024
MOE_BLOCK = 128
D_PLE = 256
RMS_EPS = 1e-6

kernel_name = "hybrid_retention_rwkv7_hmoe_step"


def rmsnorm(x, g, eps=RMS_EPS):
    xf = x.astype(jnp.float32)
    y = xf * lax.rsqrt(jnp.mean(xf * xf, -1, keepdims=True) + eps)
    return (y * g.astype(jnp.float32)).astype(x.dtype)


def split_cols(z, sizes):
    offs = np.cumsum((0,) + tuple(sizes))
    return [z[..., int(offs[j]):int(offs[j + 1])] for j in range(len(sizes))]


def rotary(x, pos):
    half = x.shape[-1] // 2
    inv = ROPE_BASE ** (-jnp.arange(half, dtype=jnp.float32) / half)
    ang = pos.astype(jnp.float32)[:, None] * inv[None, :]
    cos = jnp.cos(ang)[None, :, None, :]
    sin = jnp.sin(ang)[None, :, None, :]
    x1, x2 = x[..., :half], x[..., half:]
    return jnp.concatenate([x1 * cos - x2 * sin, x1 * sin + x2 * cos], -1)


def retention_chunked(q, k, v, s0):
    B, L = q.shape[0], q.shape[1]
    C = RET_CHUNK if L % RET_CHUNK == 0 else L
    n = L // C
    log_g = jnp.log1p(-jnp.exp2(-5.0 - jnp.arange(RET_HEADS, dtype=jnp.float32)))
    idx = jnp.arange(C, dtype=jnp.float32)
    diff = idx[:, None] - idx[None, :]
    decay_mask = jnp.where(diff >= 0, jnp.exp(log_g[:, None, None] * jnp.maximum(diff, 0.0)), 0.0)
    q_decay = jnp.exp(log_g[:, None] * (idx + 1.0))[..., None]
    k_decay = jnp.exp(log_g[:, None] * (C - 1.0 - idx))[..., None]
    chunk_decay = jnp.exp(log_g * C)[:, None, None]

    def to_chunks(t):
        return t.reshape(B, n, C, RET_HEADS, t.shape[-1]).transpose(1, 0, 3, 2, 4)

    def step(s, inp):
        qc, kc, vc = inp
        scores = jnp.einsum('bhid,bhjd->bhij', qc, kc) * decay_mask
        o = (jnp.einsum('bhij,bhje->bhie', scores, vc)
             + jnp.einsum('bhid,bhde->bhie', qc * q_decay, s))
        s = s * chunk_decay + jnp.einsum('bhjd,bhje->bhde', kc * k_decay, vc)
        return s, o

    s, o = lax.scan(step, s0, (to_chunks(q), to_chunks(k), to_chunks(v)))
    o = o.transpose(1, 0, 3, 2, 4).reshape(B, L, RET_HEADS, RET_DV)
    return o, s


def wkv7_scan(r, w, k, v, a, b, s0):
    def step(s, inp):
        rt, wt, kt, vt, at, bt = inp
        sa = jnp.einsum('bhij,bhj->bhi', s, at)
        s = s * wt[:, :, None, :] + sa[..., None] * bt[:, :, None, :] + vt[..., None] * kt[:, :, None, :]
        y = jnp.einsum('bhij,bhj->bhi', s, rt)
        return s, y

    xs = tuple(t.transpose(1, 0, 2, 3) for t in (r, w, k, v, a, b))
    s, y = lax.scan(step, s0, xs)
    return y.transpose(1, 0, 2, 3), s


def rwkv7_branch(zb, shift0, s0, mu, w0, w2, a0, a2, g2, k_k, k_a, r_k, ln_w, ln_b):
    B, L = zb.shape[0], zb.shape[1]
    prev = jnp.concatenate([shift0[:, None, :].astype(zb.dtype), zb[:, :-1]], 1)
    u = (zb + (prev - zb) * mu).astype(jnp.float32)
    r, kx, vx, wd, ad, gd = split_cols(u, (WKV_DIM, WKV_DIM, WKV_DIM, W_LORA, A_LORA, G_LORA))
    w_log = -jax.nn.softplus(-(w0 + jnp.tanh(wd) @ w2)) - 0.5
    decay = jnp.exp(-jnp.exp(w_log))
    a = jax.nn.sigmoid(a0 + ad @ a2)
    g = jax.nn.sigmoid(gd) @ g2

    def heads(t):
        return t.reshape(B, L, WKV_HEADS, WKV_N)

    kk = heads(kx * k_k)
    kk = kk / jnp.maximum(jnp.sqrt(jnp.sum(kk * kk, -1, keepdims=True)), 1e-12)
    kmod = kx * (1.0 + (a - 1.0) * k_a)
    rh, kh, vh, ah = heads(r), heads(kmod), heads(vx), heads(a)
    y, s = wkv7_scan(rh, heads(decay), kh, vh, -kk, kk * ah, s0)
    mean = jnp.mean(y, -1, keepdims=True)
    var = jnp.mean(jnp.square(y - mean), -1, keepdims=True)
    yn = ((y - mean) * lax.rsqrt(var + WKV_GN_EPS)).reshape(B, L, WKV_DIM) * ln_w + ln_b
    bonus = (jnp.sum(rh * kh * r_k, -1, keepdims=True) * vh).reshape(B, L, WKV_DIM)
    out = (yn + bonus) * g
    return out, s, zb[:, -1]


def hier_moe(h, rg_w, rg_b, re_w, re_b, e_gate, e_up, e_down):
    T = h.shape[0]
    hf = h.astype(jnp.float32)
    gp = jax.nn.softmax(hf @ rg_w.astype(jnp.float32) + rg_b.astype(jnp.float32), -1)
    g_idx = jnp.argmax(gp, -1).astype(jnp.int32)
    g_prob = jnp.max(gp, -1)
    el = (hf @ re_w.astype(jnp.float32) + re_b.astype(jnp.float32)).reshape(T, N_GROUPS, EXPERTS_PER_GROUP)
    sel = jnp.broadcast_to(g_idx[:, None, None], (T, 1, EXPERTS_PER_GROUP))
    el = jnp.take_along_axis(el, sel, axis=1)[:, 0]
    ep = jax.nn.softmax(el, -1)
    top_v, top_i = lax.top_k(ep, TOP_K)
    wts = g_prob[:, None] * top_v / jnp.sum(top_v, -1, keepdims=True)
    e_idx = g_idx[:, None] * EXPERTS_PER_GROUP + top_i.astype(jnp.int32)

    A = T * TOP_K
    flat_e = e_idx.reshape(-1)
    order = jnp.argsort(flat_e).astype(jnp.int32)
    sorted_e = flat_e[order]
    counts = jnp.zeros((N_EXPERTS,), jnp.int32).at[flat_e].add(1)
    padded = (counts + MOE_BLOCK - 1) // MOE_BLOCK * MOE_BLOCK
    pad_end = jnp.cumsum(padded)
    pad_start = pad_end - padded
    start = jnp.cumsum(counts) - counts
    dest_sorted = pad_start[sorted_e] + jnp.arange(A, dtype=jnp.int32) - start[sorted_e]
    n_blocks = -(-A // MOE_BLOCK) + N_EXPERTS
    cap = n_blocks * MOE_BLOCK
    slot_token = jnp.full((cap,), T, jnp.int32).at[dest_sorted].set(order // TOP_K)
    block_start = jnp.arange(n_blocks, dtype=jnp.int32) * MOE_BLOCK
    block_expert = jnp.minimum(jnp.searchsorted(pad_end, block_start, side='right'), N_EXPERTS - 1).astype(jnp.int32)
    h_pad = jnp.concatenate([h, jnp.zeros((1, h.shape[1]), h.dtype)], 0)
    xb = h_pad[slot_token].reshape(n_blocks, MOE_BLOCK, h.shape[1])

    def expert_block(args):
        xe, e = args
        return (jax.nn.silu(xe @ e_gate[e]) * (xe @ e_up[e])) @ e_down[e]

    yb = lax.map(expert_block, (xb, block_expert)).reshape(cap, h.shape[1])
    slot_of_assign = jnp.zeros((A,), jnp.int32).at[order].set(dest_sorted)
    y = yb[slot_of_assign].reshape(T, TOP_K, h.shape[1])
    return jnp.einsum('tkd,tk->td', y, wts.astype(y.dtype))


def trunk_layer(x, p, pos, s_ret, s_wkv, s_shift,
                g_mix, w_in, w_oa, mu, w0, w2, a0, a2, g2, k_k, k_a, r_k, ln_w, ln_b, w_ob, w_out,
                g_ffn, rg_w, rg_b, re_w, re_b, e_gate, e_up, e_down, g_ple, w_ple_gate, w_ple_proj):
    B, L = x.shape[0], x.shape[1]
    h = rmsnorm(x, g_mix)
    z = h @ w_in
    q, k, v, gr, zb, gates = split_cols(z, (RET_Q, RET_Q, RET_V, RET_V, RWKV_PROJ, 2 * D_MODEL))
    q = rotary(q.reshape(B, L, RET_HEADS, RET_DK).astype(jnp.float32), pos)
    k = rotary(k.reshape(B, L, RET_HEADS, RET_DK).astype(jnp.float32), pos) * (RET_DK ** -0.5)
    v = v.reshape(B, L, RET_HEADS, RET_DV).astype(jnp.float32)
    o, ret_new = retention_chunked(q, k, v, s_ret.astype(jnp.float32))
    o = o * lax.rsqrt(jnp.mean(o * o, -1, keepdims=True) + RMS_EPS)
    o = o.reshape(B, L, RET_V) * jax.nn.silu(gr.astype(jnp.float32))
    y_a = o.astype(x.dtype) @ w_oa
    ob, wkv_new, shift_new = rwkv7_branch(zb, s_shift, s_wkv.astype(jnp.float32), mu, w0, w2, a0, a2, g2,
                                          k_k, k_a, r_k, ln_w, ln_b)
    y_b = ob.astype(x.dtype) @ w_ob
    g_a, g_b = split_cols(gates, (D_MODEL, D_MODEL))
    x = x + (jax.nn.sigmoid(g_a) * y_a + jax.nn.sigmoid(g_b) * y_b) @ w_out
    h2 = rmsnorm(x, g_ffn)
    x = x + hier_moe(h2.reshape(B * L, D_MODEL), rg_w, rg_b, re_w, re_b, e_gate, e_up, e_down).reshape(B, L, D_MODEL)
    h3 = rmsnorm(x, g_ple)
    x = x + jax.nn.sigmoid(h3 @ w_ple_gate) * (p.astype(x.dtype) @ w_ple_proj)
    return x, ret_new, wkv_new, shift_new


def setup_inputs(seed: int = 0) -> dict:
    key = jax.random.key(seed)
    ks = iter(jax.random.split(key, 48))

    def nrm(shape, scale):
        return jax.random.normal(next(ks), shape, jnp.float32) * scale

    def gain(shape):
        return 1.0 + 0.05 * jax.random.normal(next(ks), shape, jnp.float32)

    def unif(shape, lo, hi):
        return jax.random.uniform(next(ks), shape, jnp.float32, lo, hi)

    D = D_MODEL
    return {
        'x_prompt': nrm((BATCH, SEQ, D), 1.0),
        'x_sample': nrm((DEC_BATCH, DEC_SEQ, D), 1.0),
        'state_ret': nrm((DEPTH, DEC_BATCH, RET_HEADS, RET_DK, RET_DV), 0.5),
        'state_wkv': nrm((DEPTH, DEC_BATCH, WKV_HEADS, WKV_N, WKV_N), 0.3),
        'state_shift': nrm((DEPTH, DEC_BATCH, RWKV_PROJ), 1.0),
        'p_prompt': nrm((DEPTH, BATCH, SEQ, D_PLE), 1.0),
        'p_sample': nrm((DEPTH, DEC_BATCH, DEC_SEQ, D_PLE), 1.0),
        'g_mix': gain((DEPTH, D)),
        'w_in': nrm((DEPTH, D, IN_COLS), D ** -0.5),
        'w_oa': nrm((DEPTH, RET_V, D), RET_V ** -0.5),
        'wkv_mu': unif((DEPTH, RWKV_PROJ), 0.0, 1.0),
        'wkv_w0': unif((DEPTH, WKV_DIM), -6.0, 0.0),
        'wkv_w2': nrm((DEPTH, W_LORA, WKV_DIM), 0.1),
        'wkv_a0': nrm((DEPTH, WKV_DIM), 0.1),
        'wkv_a2': nrm((DEPTH, A_LORA, WKV_DIM), A_LORA ** -0.5),
        'wkv_g2': nrm((DEPTH, G_LORA, WKV_DIM), G_LORA ** -0.5),
        'wkv_k_k': 0.85 + nrm((DEPTH, WKV_DIM), 0.05),
        'wkv_k_a': gain((DEPTH, WKV_DIM)),
        'wkv_r_k': nrm((DEPTH, WKV_HEADS, WKV_N), 0.1),
        'wkv_ln_w': gain((DEPTH, WKV_DIM)),
        'wkv_ln_b': nrm((DEPTH, WKV_DIM), 0.01),
        'w_ob': nrm((DEPTH, WKV_DIM, D), WKV_DIM ** -0.5),
        'w_out': nrm((DEPTH, D, D), D ** -0.5),
        'g_ffn': gain((DEPTH, D)),
        'router_g_w': nrm((DEPTH, D, N_GROUPS), D ** -0.5),
        'router_g_b': nrm((DEPTH, N_GROUPS), 0.01),
        'router_e_w': nrm((DEPTH, D, N_EXPERTS), D ** -0.5),
        'router_e_b': nrm((DEPTH, N_EXPERTS), 0.01),
        'e_gate': nrm((DEPTH, N_EXPERTS, D, D_EXPERT), D ** -0.5),
        'e_up': nrm((DEPTH, N_EXPERTS, D, D_EXPERT), D ** -0.5),
        'e_down': nrm((DEPTH, N_EXPERTS, D_EXPERT, D), D_EXPERT ** -0.5),
        'g_ple': gain((DEPTH, D)),
        'w_ple_gate': nrm((DEPTH, D, D), D ** -0.5),
        'w_ple_proj': nrm((DEPTH, D_PLE, D), D_PLE ** -0.5),
        'g_final': gain((D,)),
    }


def reference(x_prompt, x_sample, state_ret, state_wkv, state_shift, p_prompt, p_sample,
              g_mix, w_in, w_oa, wkv_mu, wkv_w0, wkv_w2, wkv_a0, wkv_a2, wkv_g2, wkv_k_k, wkv_k_a,
              wkv_r_k, wkv_ln_w, wkv_ln_b, w_ob, w_out, g_ffn, router_g_w, router_g_b, router_e_w,
              router_e_b, e_gate, e_up, e_down, g_ple, w_ple_gate, w_ple_proj, g_final):
    B, S = x_prompt.shape[0], x_prompt.shape[1]
    Bd, Sd = x_sample.shape[0], x_sample.shape[1]
    pos_prompt = jnp.arange(S, dtype=jnp.int32)
    pos_sample = PAST_LEN + jnp.arange(Sd, dtype=jnp.int32)
    xp, xs = x_prompt, x_sample
    ret_p, wkv_p, sh_p, ret_s, wkv_s, sh_s = [], [], [], [], [], []
    for i in range(DEPTH):
        lw = (g_mix[i], w_in[i], w_oa[i], wkv_mu[i], wkv_w0[i], wkv_w2[i], wkv_a0[i], wkv_a2[i], wkv_g2[i],
              wkv_k_k[i], wkv_k_a[i], wkv_r_k[i], wkv_ln_w[i], wkv_ln_b[i], w_ob[i], w_out[i], g_ffn[i],
              router_g_w[i], router_g_b[i], router_e_w[i], router_e_b[i], e_gate[i], e_up[i], e_down[i],
              g_ple[i], w_ple_gate[i], w_ple_proj[i])
        xp, r1, w1, s1 = trunk_layer(
            xp, p_prompt[i], pos_prompt,
            jnp.zeros((B, RET_HEADS, RET_DK, RET_DV), jnp.float32),
            jnp.zeros((B, WKV_HEADS, WKV_N, WKV_N), jnp.float32),
            jnp.zeros((B, RWKV_PROJ), xp.dtype), *lw)
        xs, r2, w2_, s2 = trunk_layer(xs, p_sample[i], pos_sample, state_ret[i], state_wkv[i], state_shift[i], *lw)
        ret_p.append(r1); wkv_p.append(w1); sh_p.append(s1)
        ret_s.append(r2); wkv_s.append(w2_); sh_s.append(s2)
    y_prompt = rmsnorm(xp, g_final)
    y_sample = rmsnorm(xs, g_final)
    return (y_prompt, y_sample, jnp.stack(ret_p), jnp.stack(wkv_p), jnp.stack(sh_p),
            jnp.stack(ret_s), jnp.stack(wkv_s), jnp.stack(sh_s))
```

```python
import functools
import math

import numpy as np
import jax
import jax.numpy as jnp
from jax import lax
from jax.experimental import pallas as pl
from jax.experimental.pallas import tpu as pltpu

F32 = jnp.float32
BF16 = jnp.bfloat16

D_MODEL = 2048
PAST_LEN = 16384
RET_HEADS = 8
RET_DK = 128
RET_DV = 256
RET_CHUNK = 128
ROPE_BASE = 10000.0
RET_Q = RET_HEADS * RET_DK
RET_V = RET_HEADS * RET_DV
WKV_HEADS = 16
WKV_N = 64
WKV_DIM = WKV_HEADS * WKV_N
WKV_PAIRS = WKV_HEADS // 2
W_LORA = 64
A_LORA = 64
G_LORA = 160
WKV_GN_EPS = 64e-5
RWKV_PROJ = 3 * WKV_DIM + W_LORA + A_LORA + G_LORA
RWKV_PAD = 4096
N_GROUPS = 4
EXPERTS_PER_GROUP = 8
N_EXPERTS = N_GROUPS * EXPERTS_PER_GROUP
TOP_K = 2
D_EXPERT = 1024
MOE_BLOCK = 128
D_PLE = 256
RMS_EPS = 1e-6
LANES = 128
WKV_CHUNK = 64

Z_COLS = RWKV_PAD + 2 * RET_Q + 2 * RET_V + 2 * D_MODEL
Z_Q = RWKV_PAD
Z_K = Z_Q + RET_Q
Z_V = Z_K + RET_Q
Z_GR = Z_V + RET_V
Z_GA = Z_GR + RET_V
Z_GB = Z_GA + D_MODEL

HI = lax.Precision.HIGHEST
VMEM_LIMIT = 56 * 1024 * 1024


def _cparams(sem):
    return pltpu.CompilerParams(dimension_semantics=sem, vmem_limit_bytes=VMEM_LIMIT)


def _tile(n, pref):
    t = min(n, pref)
    while n % t:
        t //= 2
    return t


def _dot(a, b, precision=None):
    return jnp.dot(a, b, preferred_element_type=F32, precision=precision)


def _dot_nt(a, b, precision=None):
    return lax.dot_general(a, b, (((1,), (1,)), ((), ())), preferred_element_type=F32, precision=precision)


def _dot_tn(a, b, precision=None):
    return lax.dot_general(a, b, (((0,), (0,)), ((), ())), preferred_element_type=F32, precision=precision)


def _rms(x, g):
    return x * lax.rsqrt(jnp.mean(x * x, axis=-1, keepdims=True) + RMS_EPS) * g


def _sigmoid(x):
    return 1.0 / (1.0 + jnp.exp(-x))


def _norm_matmul_kernel(x_ref, g_ref, w_ref, o_ref, h_ref):
    @pl.when(pl.program_id(1) == 0)
    def _():
        h_ref[...] = _rms(x_ref[...], g_ref[...]).astype(BF16)

    o_ref[...] = _dot(h_ref[...], w_ref[...])


def _norm_matmul(x, g, w):
    T, D = x.shape
    N = w.shape[1]
    tm = _tile(T, 1024)
    tn = _tile(N, 1024)
    return pl.pallas_call(
        _norm_matmul_kernel,
        grid=(T // tm, N // tn),
        in_specs=[pl.BlockSpec((tm, D), lambda i, j: (i, 0)),
                  pl.BlockSpec((1, D), lambda i, j: (0, 0)),
                  pl.BlockSpec((D, tn), lambda i, j: (0, j))],
        out_specs=pl.BlockSpec((tm, tn), lambda i, j: (i, j)),
        out_shape=jax.ShapeDtypeStruct((T, N), F32),
        scratch_shapes=[pltpu.VMEM((tm, D), BF16)],
        compiler_params=_cparams(("arbitrary", "arbitrary")),
        name="norm_in_proj",
    )(x, g, w)


def _ret_kernel(*refs, chunk_decay, has_state):
    if has_state:
        (q_ref, k_ref, v_ref, gr_ref, cos_ref, sin_ref, dm_ref, qd_ref, kd_ref, s0_ref,
         o_ref, s_ref, st_ref) = refs
    else:
        (q_ref, k_ref, v_ref, gr_ref, cos_ref, sin_ref, dm_ref, qd_ref, kd_ref,
         o_ref, s_ref, st_ref) = refs
    c = pl.program_id(1)

    @pl.when(c == 0)
    def _():
        if has_state:
            st_ref[...] = s0_ref[0]
        else:
            st_ref[...] = jnp.zeros_like(st_ref)

    cos = cos_ref[...]
    sin = sin_ref[...]
    half = RET_DK // 2
    for h in range(RET_HEADS):
        q = q_ref[0, :, h * RET_DK:(h + 1) * RET_DK]
        k = k_ref[0, :, h * RET_DK:(h + 1) * RET_DK]
        v = v_ref[0, :, h * RET_DV:(h + 1) * RET_DV].astype(BF16)
        qr = q * cos + pltpu.roll(q, half, 1) * sin
        kr = (k * cos + pltpu.roll(k, half, 1) * sin) * (RET_DK ** -0.5)
        scores = _dot_nt(qr.astype(BF16), kr.astype(BF16)) * dm_ref[h]
        s = st_ref[h]
        o = _dot(scores.astype(BF16), v) + _dot((qr * qd_ref[h]).astype(BF16), s.astype(BF16))
        st_ref[h] = s * chunk_decay[h] + _dot_tn((kr * kd_ref[h]).astype(BF16), v)
        o = o * lax.rsqrt(jnp.mean(o * o, axis=-1, keepdims=True) + RMS_EPS)
        gr = gr_ref[0, :, h * RET_DV:(h + 1) * RET_DV]
        o_ref[0, :, h * RET_DV:(h + 1) * RET_DV] = (o * (gr * _sigmoid(gr))).astype(BF16)

    @pl.when(c == pl.num_programs(1) - 1)
    def _():
        s_ref[0] = st_ref[...]


def _retention(z3, pos, s0, l_true):
    B, L, _ = z3.shape
    C = RET_CHUNK if l_true % RET_CHUNK == 0 else L
    c_true = RET_CHUNK if l_true % RET_CHUNK == 0 else l_true
    n = L // C
    half = RET_DK // 2
    inv = ROPE_BASE ** (-jnp.arange(half, dtype=F32) / half)
    ang = pos[:, None] * inv[None, :]
    cos = jnp.cos(ang)
    sin = jnp.sin(ang)
    cos2 = jnp.concatenate([cos, cos], -1)
    sin2 = jnp.concatenate([-sin, sin], -1)
    log_g = jnp.log1p(-jnp.exp2(-5.0 - jnp.arange(RET_HEADS, dtype=F32)))
    idx = jnp.arange(C, dtype=F32)
    diff = idx[:, None] - idx[None, :]
    dm = jnp.where(diff >= 0, jnp.exp(log_g[:, None, None] * jnp.maximum(diff, 0.0)), 0.0)
    qd = jnp.exp(log_g[:, None] * (idx + 1.0))[..., None]
    kd = jnp.where(idx < c_true, jnp.exp(log_g[:, None] * (c_true - 1.0 - idx)), 0.0)[..., None]
    chunk_decay = tuple(float(math.exp(math.log1p(-2.0 ** (-5.0 - h)) * c_true)) for h in range(RET_HEADS))

    has_state = s0 is not None
    in_specs = [
        pl.BlockSpec((1, C, RET_Q), lambda b, c: (b, c, Z_Q // RET_Q)),
        pl.BlockSpec((1, C, RET_Q), lambda b, c: (b, c, Z_K // RET_Q)),
        pl.BlockSpec((1, C, RET_V), lambda b, c: (b, c, Z_V // RET_V)),
        pl.BlockSpec((1, C, RET_V), lambda b, c: (b, c, Z_GR // RET_V)),
        pl.BlockSpec((C, RET_DK), lambda b, c: (c, 0)),
        pl.BlockSpec((C, RET_DK), lambda b, c: (c, 0)),
        pl.BlockSpec((RET_HEADS, C, C), lambda b, c: (0, 0, 0)),
        pl.BlockSpec((RET_HEADS, C, 1), lambda b, c: (0, 0, 0)),
        pl.BlockSpec((RET_HEADS, C, 1), lambda b, c: (0, 0, 0)),
    ]
    args = [z3, z3, z3, z3, cos2, sin2, dm, qd, kd]
    if has_state:
        in_specs.append(pl.BlockSpec((1, RET_HEADS, RET_DK, RET_DV), lambda b, c: (b, 0, 0, 0)))
        args.append(s0)
    return pl.pallas_call(
        functools.partial(_ret_kernel, chunk_decay=chunk_decay, has_state=has_state),
        grid=(B, n),
        in_specs=in_specs,
        out_specs=[pl.BlockSpec((1, C, RET_V), lambda b, c: (b, c, 0)),
                   pl.BlockSpec((1, RET_HEADS, RET_DK, RET_DV), lambda b, c: (b, 0, 0, 0))],
        out_shape=[jax.ShapeDtypeStruct((B, L, RET_V), BF16),
                   jax.ShapeDtypeStruct((B, RET_HEADS, RET_DK, RET_DV), F32)],
        scratch_shapes=[pltpu.VMEM((RET_HEADS, RET_DK, RET_DV), F32)],
        compiler_params=_cparams(("arbitrary", "arbitrary")),
        name="retention",
    )(*args)


def _rwkv_kernel(*refs, C, l_true, has_state):
    if has_state:
        (z_ref, sh_ref, s0_ref, mu_ref, w0_ref, a0_ref, kk_ref, ka_ref, rk_ref, lnw_ref, lnb_ref,
         w2_ref, a2_ref, g2_ref, ob_ref, so_ref, st_ref, carry_ref) = refs
    else:
        (z_ref, mu_ref, w0_ref, a0_ref, kk_ref, ka_ref, rk_ref, lnw_ref, lnb_ref,
         w2_ref, a2_ref, g2_ref, ob_ref, so_ref, st_ref, carry_ref) = refs
    c = pl.program_id(1)
    N = WKV_N
    C2 = 2 * C

    @pl.when(c == 0)
    def _():
        if has_state:
            carry_ref[...] = sh_ref[0]
            zero = jnp.zeros((N, N), F32)
            for p in range(WKV_PAIRS):
                top = jnp.concatenate([s0_ref[0, 2 * p], zero], axis=1)
                bot = jnp.concatenate([zero, s0_ref[0, 2 * p + 1]], axis=1)
                st_ref[p] = jnp.concatenate([top, bot], axis=0)
        else:
            carry_ref[...] = jnp.zeros_like(carry_ref)
            st_ref[...] = jnp.zeros_like(st_ref)

    zb = z_ref[0]
    row = lax.broadcasted_iota(jnp.int32, (C, 1), 0)
    prev = jnp.where(row == 0, carry_ref[...], pltpu.roll(zb, 1, 0))
    carry_ref[...] = zb[C - 1:C, :]
    u = zb + (prev - zb) * mu_ref[...]
    r = u[:, 0:WKV_DIM]
    kx = u[:, WKV_DIM:2 * WKV_DIM]
    vx = u[:, 2 * WKV_DIM:3 * WKV_DIM]
    wa = u[:, 3 * WKV_DIM:3 * WKV_DIM + LANES]
    gd = u[:, 3 * WKV_DIM + LANES:3 * WKV_DIM + 3 * LANES]
    lw = _dot(jnp.tanh(wa).astype(BF16), w2_ref[...])
    la = _dot(wa.astype(BF16), a2_ref[...])
    g = _dot(_sigmoid(gd).astype(BF16), g2_ref[...])
    xw = -(w0_ref[...] + lw)
    softplus = jnp.maximum(xw, 0.0) + jnp.log(1.0 + jnp.exp(-jnp.abs(xw)))
    logw = -jnp.exp(-softplus - 0.5)
    alpha = _sigmoid(a0_ref[...] + la)
    kk = kx * kk_ref[...]
    valid = None
    if l_true < C:
        valid = row < l_true
        logw = jnp.where(valid, logw, 0.0)

    ri = lax.broadcasted_iota(jnp.int32, (C, C), 0)
    ci = lax.broadcasted_iota(jnp.int32, (C, C), 1)
    cum = _dot((ri >= ci).astype(F32), logw, HI)

    r2 = lax.broadcasted_iota(jnp.int32, (C2, C2), 0)
    c2 = lax.broadcasted_iota(jnp.int32, (C2, C2), 1)
    strict = r2 > c2
    incl = r2 >= c2
    eye = (r2 == c2).astype(F32)
    bi = lax.broadcasted_iota(jnp.int32, (LANES, LANES), 0) // N
    bj = lax.broadcasted_iota(jnp.int32, (LANES, LANES), 1) // N
    seg = (bi == bj).astype(F32)
    lane_a = lax.broadcasted_iota(jnp.int32, (1, LANES), 1) < N

    def seg_sum(x):
        return _dot(x, seg, HI)

    def stack(x):
        return jnp.concatenate([jnp.where(lane_a, x, 0.0), jnp.where(lane_a, 0.0, x)], axis=0)

    n_sq = int(round(math.log2(C))) - 1
    for p in range(WKV_PAIRS):
        sl = slice(p * LANES, (p + 1) * LANES)
        kkp = kk[:, sl]
        kkn = kkp / jnp.maximum(jnp.sqrt(seg_sum(kkp * kkp)), 1e-12)
        al = alpha[:, sl]
        kmod = kx[:, sl] * (1.0 + (al - 1.0) * ka_ref[:, sl])
        rp = r[:, sl]
        vp = vx[:, sl]
        at = -kkn
        bt = kkn * al
        kt = kmod
        if valid is not None:
            at = jnp.where(valid, at, 0.0)
            bt = jnp.where(valid, bt, 0.0)
            kt = jnp.where(valid, kt, 0.0)
            vt = jnp.where(valid, vp, 0.0)
        else:
            vt = vp
        cl = cum[:, sl]
        p_in = jnp.exp(cl)
        p_prev = jnp.exp(cl - logw[:, sl])
        p_inv = jnp.exp(-cl)
        p_end = jnp.exp(cl[C - 1:C, :])
        xa = stack(at * p_prev)
        xr = stack(rp * p_in)
        xb = stack(bt * p_inv)
        xk = stack(kt * p_inv)
        xv = stack(vt)
        lhs = jnp.concatenate([xa, xr], axis=0)
        m2 = st_ref[p]
        gab = _dot_nt(lhs, xb, HI)
        gak = _dot_nt(lhs, xk, HI)
        l_ab = jnp.where(strict, gab[0:C2], 0.0)
        l_ak = jnp.where(strict, gak[0:C2], 0.0)
        t_rb = jnp.where(incl, gab[C2:2 * C2], 0.0)
        t_rk = jnp.where(incl, gak[C2:2 * C2], 0.0)
        xs = _dot_nt(lhs, m2, HI)
        rhs = xs[0:C2] + _dot(l_ak, xv, HI)
        tinv = eye + l_ab
        lp = l_ab
        for _ in range(n_sq):
            lp = _dot(lp, lp, HI)
            tinv = tinv + _dot(tinv, lp, HI)
        um = _dot(tinv, rhs, HI)
        ym = xs[C2:2 * C2] + _dot(t_rb, um, HI) + _dot(t_rk, xv, HI)
        y = ym[0:C] + ym[C:C2]
        st_ref[p] = m2 * p_end + _dot_tn(um, xb * p_end, HI) + _dot_tn(xv, xk * p_end, HI)

        mean = seg_sum(y) * (1.0 / N)
        d = y - mean
        var = seg_sum(d * d) * (1.0 / N)
        yn = d * lax.rsqrt(var + WKV_GN_EPS) * lnw_ref[:, sl] + lnb_ref[:, sl]
        bonus = seg_sum(rp * kmod * rk_ref[:, sl]) * vp
        ob_ref[0, :, sl] = ((yn + bonus) * g[:, sl]).astype(BF16)

    @pl.when(c == pl.num_programs(1) - 1)
    def _():
        for p in range(WKV_PAIRS):
            m2 = st_ref[p]
            so_ref[0, 2 * p] = m2[0:N, 0:N]
            so_ref[0, 2 * p + 1] = m2[N:2 * N, N:2 * N]


def _rwkv(z3, shift0, s0, params, l_true):
    B, L, _ = z3.shape
    C = _tile(L, WKV_CHUNK)
    n = L // C
    has_state = s0 is not None
    vec = lambda width: pl.BlockSpec((1, width), lambda b, c: (0, 0))
    in_specs = [pl.BlockSpec((1, C, RWKV_PAD), lambda b, c: (b, c, 0))]
    args = [z3]
    if has_state:
        in_specs += [pl.BlockSpec((1, 1, RWKV_PAD), lambda b, c: (b, 0, 0)),
                     pl.BlockSpec((1, WKV_HEADS, WKV_N, WKV_N), lambda b, c: (b, 0, 0, 0))]
        args += [shift0, s0]
    in_specs += [vec(RWKV_PAD)] + [vec(WKV_DIM)] * 7
    in_specs += [pl.BlockSpec((LANES, WKV_DIM), lambda b, c: (0, 0)),
                 pl.BlockSpec((LANES, WKV_DIM), lambda b, c: (0, 0)),
                 pl.BlockSpec((2 * LANES, WKV_DIM), lambda b, c: (0, 0))]
    args += list(params)
    return pl.pallas_call(
        functools.partial(_rwkv_kernel, C=C, l_true=l_true, has_state=has_state),
        grid=(B, n),
        in_specs=in_specs,
        out_specs=[pl.BlockSpec((1, C, WKV_DIM), lambda b, c: (b, c, 0)),
                   pl.BlockSpec((1, WKV_HEADS, WKV_N, WKV_N), lambda b, c: (b, 0, 0, 0))],
        out_shape=[jax.ShapeDtypeStruct((B, L, WKV_DIM), BF16),
                   jax.ShapeDtypeStruct((B, WKV_HEADS, WKV_N, WKV_N), F32)],
        scratch_shapes=[pltpu.VMEM((WKV_PAIRS, LANES, LANES), F32),
                        pltpu.VMEM((1, RWKV_PAD), F32)],
        compiler_params=_cparams(("arbitrary", "arbitrary")),
        name="rwkv7",
    )(*args)


def _merge_kernel(o_ref, ob_ref, ga_ref, gb_ref, woa_ref, wob_ref, m_ref):
    ya = _dot(o_ref[...], woa_ref[...])
    yb = _dot(ob_ref[...], wob_ref[...])
    m_ref[...] = (_sigmoid(ga_ref[...]) * ya + _sigmoid(gb_ref[...]) * yb).astype(BF16)


def _merge(o, ob, gates, ga_off, gb_off, w_oa, w_ob):
    T = o.shape[0]
    tm = _tile(T, 512)
    tn = 1024
    nj = D_MODEL // tn
    return pl.pallas_call(
        _merge_kernel,
        grid=(nj, T // tm),
        in_specs=[pl.BlockSpec((tm, RET_V), lambda j, i: (i, 0)),
                  pl.BlockSpec((tm, WKV_DIM), lambda j, i: (i, 0)),
                  pl.BlockSpec((tm, tn), lambda j, i: (i, ga_off // tn + j)),
                  pl.BlockSpec((tm, tn), lambda j, i: (i, gb_off // tn + j)),
                  pl.BlockSpec((RET_V, tn), lambda j, i: (0, j)),
                  pl.BlockSpec((WKV_DIM, tn), lambda j, i: (0, j))],
        out_specs=pl.BlockSpec((tm, tn), lambda j, i: (i, j)),
        out_shape=jax.ShapeDtypeStruct((T, D_MODEL), BF16),
        compiler_params=_cparams(("arbitrary", "arbitrary")),
        name="merge",
    )(o, ob, gates, gates, w_oa, w_ob)


R_E1, R_E2, R_RANK1, R_RANK2, R_W1, R_W2 = range(6)


def _outproj_router_kernel(mp_ref, xp_ref, ms_ref, xs_ref, wout_ref, g_ref, wr_ref, br_ref,
                           x1_ref, h2_ref, rt_ref, cnt_ref, carry_ref, *, n_prompt_tiles):
    i = pl.program_id(0)
    tm = mp_ref.shape[0]

    @pl.when(i == 0)
    def _():
        carry_ref[...] = jnp.zeros_like(carry_ref)

    def body(m_ref, x_ref):
        x1 = x_ref[...] + _dot(m_ref[...], wout_ref[...])
        x1_ref[...] = x1
        h2 = _rms(x1, g_ref[...])
        h2_ref[...] = h2
        logits = _dot(h2, wr_ref[...], HI) + br_ref[...]
        col = lax.broadcasted_iota(jnp.int32, (tm, LANES), 1).astype(F32)
        neg = -1e30
        big = 1e4
        lg = jnp.where(col < N_GROUPS, logits, neg)
        gmax = jnp.max(lg, axis=-1, keepdims=True)
        gidx = jnp.min(jnp.where(lg == gmax, col, big), axis=-1, keepdims=True)
        gprob = 1.0 / jnp.sum(jnp.where(col < N_GROUPS, jnp.exp(lg - gmax), 0.0), axis=-1, keepdims=True)
        lo = N_GROUPS + EXPERTS_PER_GROUP * gidx
        le = jnp.where((col >= lo) & (col < lo + EXPERTS_PER_GROUP), logits, neg)
        m1 = jnp.max(le, axis=-1, keepdims=True)
        i1 = jnp.min(jnp.where(le == m1, col, big), axis=-1, keepdims=True)
        le2 = jnp.where(col == i1, neg, le)
        m2 = jnp.max(le2, axis=-1, keepdims=True)
        i2 = jnp.min(jnp.where(le2 == m2, col, big), axis=-1, keepdims=True)
        e2 = jnp.exp(m2 - m1)
        w1 = gprob / (1.0 + e2)
        w2 = gprob * e2 / (1.0 + e2)
        ex1 = i1 - N_GROUPS
        ex2 = i2 - N_GROUPS
        oh1 = (col == ex1).astype(F32)
        oh2 = (col == ex2).astype(F32)
        ohs = oh1 + oh2
        ri = lax.broadcasted_iota(jnp.int32, (tm, tm), 0)
        ci = lax.broadcasted_iota(jnp.int32, (tm, tm), 1)
        prefix = _dot((ri > ci).astype(BF16), ohs.astype(BF16)) + carry_ref[...]
        rank1 = jnp.sum(prefix * oh1, axis=-1, keepdims=True)
        rank2 = jnp.sum(prefix * oh2, axis=-1, keepdims=True)
        carry_ref[...] = carry_ref[...] + jnp.sum(ohs, axis=0, keepdims=True)
        rec = jnp.zeros((tm, LANES), F32)
        for lane, val in ((R_E1, ex1), (R_E2, ex2), (R_RANK1, rank1), (R_RANK2, rank2), (R_W1, w1), (R_W2, w2)):
            rec = jnp.where(col == lane, val, rec)
        rt_ref[...] = rec
        cnt_ref[...] = carry_ref[...]

    @pl.when(i < n_prompt_tiles)
    def _():
        body(mp_ref, xp_ref)

    @pl.when(i >= n_prompt_tiles)
    def _():
        body(ms_ref, xs_ref)


def _outproj_router(m_p, x_p, m_s, x_s, w_out, g_ffn, w_r, b_r):
    t_p, t_s = m_p.shape[0], m_s.shape[0]
    t_all = t_p + t_s
    tm = _tile(t_s, 256)
    assert t_p % tm == 0
    n_p = t_p // tm
    p_idx = lambda i: (jnp.minimum(i, n_p - 1), 0)
    s_idx = lambda i: (jnp.maximum(i - n_p, 0), 0)
    return pl.pallas_call(
        functools.partial(_outproj_router_kernel, n_prompt_tiles=n_p),
        grid=(t_all // tm,),
        in_specs=[pl.BlockSpec((tm, D_MODEL), p_idx),
                  pl.BlockSpec((tm, D_MODEL), p_idx),
                  pl.BlockSpec((tm, D_MODEL), s_idx),
                  pl.BlockSpec((tm, D_MODEL), s_idx),
                  pl.BlockSpec((D_MODEL, D_MODEL), lambda i: (0, 0)),
                  pl.BlockSpec((1, D_MODEL), lambda i: (0, 0)),
                  pl.BlockSpec((D_MODEL, LANES), lambda i: (0, 0)),
                  pl.BlockSpec((1, LANES), lambda i: (0, 0))],
        out_specs=[pl.BlockSpec((tm, D_MODEL), lambda i: (i, 0)),
                   pl.BlockSpec((tm, D_MODEL), lambda i: (i, 0)),
                   pl.BlockSpec((tm, LANES), lambda i: (i, 0)),
                   pl.BlockSpec((1, LANES), lambda i: (0, 0))],
        out_shape=[jax.ShapeDtypeStruct((t_all, D_MODEL), F32),
                   jax.ShapeDtypeStruct((t_all, D_MODEL), F32),
                   jax.ShapeDtypeStruct((t_all, LANES), F32),
                   jax.ShapeDtypeStruct((1, LANES), F32)],
        scratch_shapes=[pltpu.VMEM((1, LANES), F32)],
        compiler_params=_cparams(("arbitrary",)),
        name="outproj_router",
    )(m_p, x_p, m_s, x_s, w_out, g_ffn, w_r, b_r)


def _moe_kernel(be_ref, nu_ref, tok_ref, h_hbm, wg_ref, wu_ref, wd_ref, y_ref, xbuf, sem):
    b = pl.program_id(0)

    def row_copy(tok, r):
        return pltpu.make_async_copy(h_hbm.at[pl.ds(tok, 1), :], xbuf.at[pl.ds(r, 1), :], sem)

    @pl.when(b < nu_ref[0])
    def _():
        def issue(r, carry):
            row_copy(tok_ref[b * MOE_BLOCK + r], r).start()
            return carry

        lax.fori_loop(0, MOE_BLOCK, issue, 0)

        def drain(r, carry):
            row_copy(0, r).wait()
            return carry

        lax.fori_loop(0, MOE_BLOCK, drain, 0)
        x = xbuf[...].astype(BF16)
        gate = _dot(x, wg_ref[0])
        up = _dot(x, wu_ref[0])
        act = (gate * _sigmoid(gate) * up).astype(BF16)
        y_ref[...] = _dot(act, wd_ref[0])

    @pl.when(b >= nu_ref[0])
    def _():
        y_ref[...] = jnp.zeros_like(y_ref)


def _moe(block_expert, n_used, slot_token, h2, e_gate, e_up, e_down):
    n_blocks = block_expert.shape[0]
    grid_spec = pltpu.PrefetchScalarGridSpec(
        num_scalar_prefetch=3,
        grid=(n_blocks,),
        in_specs=[pl.BlockSpec(memory_space=pl.ANY),
                  pl.BlockSpec((1, D_MODEL, D_EXPERT), lambda b, be, nu, tk: (be[b], 0, 0)),
                  pl.BlockSpec((1, D_MODEL, D_EXPERT), lambda b, be, nu, tk: (be[b], 0, 0)),
                  pl.BlockSpec((1, D_EXPERT, D_MODEL), lambda b, be, nu, tk: (be[b], 0, 0))],
        out_specs=pl.BlockSpec((MOE_BLOCK, D_MODEL), lambda b, be, nu, tk: (b, 0)),
        scratch_shapes=[pltpu.VMEM((MOE_BLOCK, D_MODEL), F32), pltpu.SemaphoreType.DMA(())],
    )
    return pl.pallas_call(
        _moe_kernel,
        grid_spec=grid_spec,
        out_shape=jax.ShapeDtypeStruct((n_blocks * MOE_BLOCK, D_MODEL), F32),
        compiler_params=_cparams(("arbitrary",)),
        name="moe_experts",
    )(block_expert, n_used, slot_token, h2, e_gate, e_up, e_down)


def _combine_ple_kernel(dest_ref, x1_ref, rt_ref, p_ref, yb_hbm, gple_ref, wgate_ref, wproj_ref, gfin_ref,
                        out_ref, ybuf, sem, *, row_off):
    i = pl.program_id(0)
    tm = x1_ref.shape[0]
    base = row_off + i * tm

    def row_copy(slot, k, r):
        return pltpu.make_async_copy(yb_hbm.at[pl.ds(slot, 1), :], ybuf.at[k, pl.ds(r, 1), :], sem)

    def issue(r, carry):
        for k in range(TOP_K):
            row_copy(dest_ref[TOP_K * (base + r) + k], k, r).start()
        return carry

    lax.fori_loop(0, tm, issue, 0)

    def drain(r, carry):
        for k in range(TOP_K):
            row_copy(0, k, r).wait()
        return carry

    lax.fori_loop(0, tm, drain, 0)
    rt = rt_ref[...]
    w1 = rt[:, R_W1:R_W1 + 1]
    w2 = rt[:, R_W2:R_W2 + 1]
    x2 = x1_ref[...] + (w1 * ybuf[0] + w2 * ybuf[1])
    h3 = _rms(x2, gple_ref[...]).astype(BF16)
    gate = _sigmoid(_dot(h3, wgate_ref[...]))
    x3 = x2 + gate * _dot(p_ref[...].astype(BF16), wproj_ref[...])
    out_ref[...] = _rms(x3, gfin_ref[...])


def _combine_ple(dest, x1, route, p, yb, g_ple, w_gate, w_proj, g_final, row_off, T):
    tm = _tile(T, 256)
    ob = row_off // tm
    grid_spec = pltpu.PrefetchScalarGridSpec(
        num_scalar_prefetch=1,
        grid=(T // tm,),
        in_specs=[pl.BlockSpec((tm, D_MODEL), lambda i, d: (i + ob, 0)),
                  pl.BlockSpec((tm, LANES), lambda i, d: (i + ob, 0)),
                  pl.BlockSpec((tm, D_PLE), lambda i, d: (i, 0)),
                  pl.BlockSpec(memory_space=pl.ANY),
                  pl.BlockSpec((1, D_MODEL), lambda i, d: (0, 0)),
                  pl.BlockSpec((D_MODEL, D_MODEL), lambda i, d: (0, 0)),
                  pl.BlockSpec((D_PLE, D_MODEL), lambda i, d: (0, 0)),
                  pl.BlockSpec((1, D_MODEL), lambda i, d: (0, 0))],
        out_specs=pl.BlockSpec((tm, D_MODEL), lambda i, d: (i, 0)),
        scratch_shapes=[pltpu.VMEM((TOP_K, tm, D_MODEL), F32), pltpu.SemaphoreType.DMA(())],
    )
    return pl.pallas_call(
        functools.partial(_combine_ple_kernel, row_off=row_off),
        grid_spec=grid_spec,
        out_shape=jax.ShapeDtypeStruct((T, D_MODEL), F32),
        compiler_params=_cparams(("arbitrary",)),
        name="combine_ple",
    )(dest, x1, route, p, yb, g_ple, w_gate, w_proj, g_final)


def _pad_rows(w, rows_before, rows_total):
    return jnp.pad(w, ((rows_before, rows_total - rows_before - w.shape[0]), (0, 0)))


def _layer(xp, xs, p_p, p_s, s_ret, s_wkv, s_shift, g_final,
           g_mix, w_in, w_oa, mu, w0, w2, a0, a2, g2, k_k, k_a, r_k, ln_w, ln_b, w_ob, w_out,
           g_ffn, rg_w, rg_b, re_w, re_b, e_gate, e_up, e_down, g_ple, w_ple_gate, w_ple_proj):
    B, S, D = xp.shape
    Bd, Sd, _ = xs.shape
    sd_pad = -(-Sd // 8) * 8
    t_p, t_s = B * S, Bd * Sd
    t_all = t_p + t_s
    row = lambda v: v.reshape(1, -1).astype(F32)

    ret_w = w_in[:, 0:2 * RET_Q + 2 * RET_V]
    rw_w = w_in[:, 2 * RET_Q + 2 * RET_V:2 * RET_Q + 2 * RET_V + RWKV_PROJ]
    gate_w = w_in[:, 2 * RET_Q + 2 * RET_V + RWKV_PROJ:]
    w_z = jnp.concatenate([jnp.pad(rw_w, ((0, 0), (0, RWKV_PAD - RWKV_PROJ))), ret_w, gate_w], axis=1).astype(BF16)
    mu_p = jnp.pad(mu, (0, RWKV_PAD - RWKV_PROJ)).reshape(1, RWKV_PAD)
    rwkv_params = (mu_p, row(w0), row(a0), row(k_k), row(k_a), row(r_k), row(ln_w), row(ln_b),
                   _pad_rows(w2, 0, LANES).astype(BF16),
                   _pad_rows(a2, W_LORA, LANES).astype(BF16),
                   _pad_rows(g2, 0, 2 * LANES).astype(BF16))
    w_oa_b, w_ob_b, w_out_b = w_oa.astype(BF16), w_ob.astype(BF16), w_out.astype(BF16)
    w_r = jnp.pad(jnp.concatenate([rg_w, re_w], axis=1), ((0, 0), (0, LANES - N_GROUPS - N_EXPERTS)))
    b_r = jnp.pad(jnp.concatenate([rg_b, re_b]), (0, LANES - N_GROUPS - N_EXPERTS)).reshape(1, LANES)
    e_gate_b, e_up_b, e_down_b = e_gate.astype(BF16), e_up.astype(BF16), e_down.astype(BF16)
    w_pg_b, w_pp_b = w_ple_gate.astype(BF16), w_ple_proj.astype(BF16)

    z_p = _norm_matmul(xp.reshape(t_p, D), row(g_mix), w_z)
    z_p3 = z_p.reshape(B, S, Z_COLS)
    o_p, ret_p = _retention(z_p3, jnp.arange(S, dtype=F32), None, S)
    ob_p, wkv_p = _rwkv(z_p3, None, None, rwkv_params, S)
    shift_p = z_p3[:, S - 1, 0:RWKV_PROJ]
    m_p = _merge(o_p.reshape(t_p, RET_V), ob_p.reshape(t_p, WKV_DIM), z_p, Z_GA, Z_GB, w_oa_b, w_ob_b)

    xs_pad = jnp.pad(xs, ((0, 0), (0, sd_pad - Sd), (0, 0)))
    z_s = _norm_matmul(xs_pad.reshape(Bd * sd_pad, D), row(g_mix), w_z)
    z_s3 = z_s.reshape(Bd, sd_pad, Z_COLS)
    pos_s = (PAST_LEN + jnp.arange(sd_pad, dtype=jnp.int32)).astype(F32)
    o_s, ret_s = _retention(z_s3, pos_s, s_ret, Sd)
    sh0 = jnp.pad(s_shift, ((0, 0), (0, RWKV_PAD - RWKV_PROJ))).reshape(Bd, 1, RWKV_PAD)
    ob_s, wkv_s = _rwkv(z_s3, sh0, s_wkv, rwkv_params, Sd)
    shift_s = z_s3[:, Sd - 1, 0:RWKV_PROJ]
    gates_s = z_s3[:, :Sd, Z_GA:].reshape(t_s, 2 * D_MODEL)
    m_s = _merge(o_s[:, :Sd].reshape(t_s, RET_V), ob_s[:, :Sd].reshape(t_s, WKV_DIM), gates_s, 0, D_MODEL,
                 w_oa_b, w_ob_b)

    x1, h2, route, cnt = _outproj_router(m_p, xp.reshape(t_p, D), m_s, xs.reshape(t_s, D), w_out_b, row(g_ffn),
                                         w_r, b_r)

    A = t_all * TOP_K
    counts = cnt[0, :N_EXPERTS].astype(jnp.int32)
    padded = (counts + MOE_BLOCK - 1) // MOE_BLOCK * MOE_BLOCK
    pad_end = jnp.cumsum(padded)
    pad_start = pad_end - padded
    e_idx = route[:, R_E1:R_E2 + 1].astype(jnp.int32)
    rank = route[:, R_RANK1:R_RANK2 + 1].astype(jnp.int32)
    dest = (pad_start[e_idx] + rank).reshape(A)
    n_blocks = -(-A // MOE_BLOCK) + N_EXPERTS
    slot_token = jnp.zeros((n_blocks * MOE_BLOCK,), jnp.int32).at[dest].set(jnp.arange(A, dtype=jnp.int32) // TOP_K)
    n_used = pad_end[-1] // MOE_BLOCK
    blk = jnp.arange(n_blocks, dtype=jnp.int32)
    blk = jnp.minimum(blk, n_used - 1)
    block_expert = jnp.minimum(jnp.searchsorted(pad_end, blk * MOE_BLOCK, side='right'), N_EXPERTS - 1).astype(jnp.int32)
    yb = _moe(block_expert, n_used.reshape(1).astype(jnp.int32), slot_token, h2, e_gate_b, e_up_b, e_down_b)

    y_p = _combine_ple(dest, x1, route, p_p.reshape(t_p, D_PLE), yb, row(g_ple), w_pg_b, w_pp_b, row(g_final), 0, t_p)
    y_s = _combine_ple(dest, x1, route, p_s.reshape(t_s, D_PLE), yb, row(g_ple), w_pg_b, w_pp_b, row(g_final), t_p, t_s)
    return (y_p.reshape(B, S, D), y_s.reshape(Bd, Sd, D), ret_p, wkv_p, shift_p, ret_s, wkv_s, shift_s)


def kernel(x_prompt, x_sample, state_ret, state_wkv, state_shift, p_prompt, p_sample, g_mix, w_in, w_oa, wkv_mu, wkv_w0, wkv_w2, wkv_a0, wkv_a2, wkv_g2, wkv_k_k, wkv_k_a, wkv_r_k, wkv_ln_w, wkv_ln_b, w_ob, w_out, g_ffn, router_g_w, router_g_b, router_e_w, router_e_b, e_gate, e_up, e_down, g_ple, w_ple_gate, w_ple_proj, g_final):
    assert g_mix.shape[0] == 1, "single-layer trunk"
    lw = (g_mix[0], w_in[0], w_oa[0], wkv_mu[0], wkv_w0[0], wkv_w2[0], wkv_a0[0], wkv_a2[0], wkv_g2[0],
          wkv_k_k[0], wkv_k_a[0], wkv_r_k[0], wkv_ln_w[0], wkv_ln_b[0], w_ob[0], w_out[0], g_ffn[0],
          router_g_w[0], router_g_b[0], router_e_w[0], router_e_b[0], e_gate[0], e_up[0], e_down[0],
          g_ple[0], w_ple_gate[0], w_ple_proj[0])
    y_p, y_s, ret_p, wkv_p, sh_p, ret_s, wkv_s, sh_s = _layer(
        x_prompt, x_sample, p_prompt[0], p_sample[0], state_ret[0], state_wkv[0], state_shift[0], g_final, *lw)
    return (y_p, y_s, ret_p[None], wkv_p[None], sh_p[None], ret_s[None], wkv_s[None], sh_s[None])
```
